```python
import math
import jax
import jax.numpy as jnp
from jax import lax
import numpy as np

D_MODEL = 1024
BATCH = 8
SEQ = 2048
DEPTH = 1

N_META = 16
EPS = 1e-6
DA_HEADS = 4
DA_HEAD_DIM = 64
DA_V_DIM = 2 * DA_HEAD_DIM
DA_WIDTH = DA_HEADS * DA_V_DIM
ROPE_THETA = 500000.0
ROPE_DIM = DA_HEAD_DIM // 4
Q_BLOCK = 128
GLA_HEADS = 4
GLA_DK = 64
GLA_DV = 128
GLA_WIDTH = GLA_HEADS * GLA_DV
GLA_GATE_RANK = 16
GLA_TAU = 16.0
GLA_CHUNK = 64
MIX_WIDTH = DA_WIDTH + GLA_WIDTH
IN_SIZES = (DA_HEADS * 2 * DA_HEAD_DIM, DA_HEADS * 2 * DA_HEAD_DIM, DA_WIDTH, GLA_HEADS * GLA_DK, GLA_HEADS * GLA_DK, GLA_WIDTH, GLA_WIDTH, GLA_GATE_RANK, GLA_GATE_RANK)
IN_WIDTH = sum(IN_SIZES)
PEER_HEADS = 8
PEER_NKEYS = 128
PEER_EXPERTS = PEER_NKEYS * PEER_NKEYS
PEER_KEY_DIM = 128
PEER_TOPK = 16
PEER_BLOCK = 256

kernel_name = 'hymba_diffattn_gla_peer_encoder'


def rmsnorm(x, w):
    xf = x.astype(jnp.float32)
    y = xf * lax.rsqrt(jnp.mean(xf * xf, axis=-1, keepdims=True) + EPS)
    return (y * w.astype(jnp.float32)).astype(x.dtype)


def partial_rope(x, pos):
    inv = jnp.power(ROPE_THETA, -jnp.arange(0, ROPE_DIM, 2, dtype=jnp.float32) / ROPE_DIM)
    ang = pos.astype(jnp.float32)[:, None] * inv[None, :]
    cos = jnp.cos(ang)[None, :, None, None, :]
    sin = jnp.sin(ang)[None, :, None, None, :]
    xr = x[..., :ROPE_DIM].astype(jnp.float32)
    x1, x2 = xr[..., :ROPE_DIM // 2], xr[..., ROPE_DIM // 2:]
    rot = jnp.concatenate([x1 * cos - x2 * sin, x2 * cos + x1 * sin], axis=-1)
    return jnp.concatenate([rot.astype(x.dtype), x[..., ROPE_DIM:]], axis=-1)


def diff_attention(q, k, v, lam, pos):
    B, T = q.shape[0], q.shape[1]
    q = partial_rope(q, pos) * (DA_HEAD_DIM ** -0.5)
    k = partial_rope(k, pos)
    nb = -(-T // Q_BLOCK)
    qp = jnp.pad(q, ((0, 0), (0, nb * Q_BLOCK - T), (0, 0), (0, 0), (0, 0)))
    qb = qp.reshape(B, nb, Q_BLOCK, DA_HEADS, 2, DA_HEAD_DIM).transpose(1, 0, 2, 3, 4, 5)

    def block(qi):
        s = jnp.einsum('bqhcd,bkhcd->bhcqk', qi, k).astype(jnp.float32)
        p = jax.nn.softmax(s, axis=-1)
        a = p[:, :, 0] - lam * p[:, :, 1]
        return jnp.einsum('bhqk,bkhv->bqhv', a.astype(v.dtype), v)

    o = lax.map(block, qb)
    return o.transpose(1, 0, 2, 3, 4).reshape(B, nb * Q_BLOCK, DA_HEADS, DA_V_DIM)[:, :T]


def gla_chunk_scan(q, k, v, g):
    B, L = q.shape[0], q.shape[1]
    n = L // GLA_CHUNK

    def to_chunks(a):
        return a.reshape(B, n, GLA_CHUNK, a.shape[2], a.shape[3]).transpose(1, 0, 3, 2, 4)

    mask = jnp.tril(jnp.ones((GLA_CHUNK, GLA_CHUNK), dtype=bool))[:, :, None]

    def step(state, inp):
        qc, kc, vc, gc = inp
        b = jnp.cumsum(gc, axis=2)
        rel = b[:, :, :, None, :] - b[:, :, None, :, :]
        decay = jnp.exp(jnp.where(mask, rel, -jnp.inf))
        attn = jnp.einsum('bhid,bhjd,bhijd->bhij', qc, kc, decay)
        o = jnp.einsum('bhij,bhjv->bhiv', attn, vc) + jnp.einsum('bhid,bhdv->bhiv', qc * jnp.exp(b), state)
        b_last = b[:, :, -1]
        state = state * jnp.exp(b_last)[..., None] + jnp.einsum('bhjd,bhjv->bhdv', kc * jnp.exp(b_last[:, :, None] - b), vc)
        return state, o

    state0 = jnp.zeros((B, q.shape[2], q.shape[3], v.shape[3]), jnp.float32)
    _, o = lax.scan(step, state0, (to_chunks(q), to_chunks(k), to_chunks(v), to_chunks(g)))
    return o.transpose(1, 0, 3, 2, 4).reshape(B, L, q.shape[2], v.shape[3])


def gla_bidirectional(q, k, v, g_f, g_b):
    pad = (-N_META) % GLA_CHUNK
    padf = lambda a: jnp.pad(a, ((0, 0), (pad, 0), (0, 0), (0, 0)))
    flip = lambda a: a[:, ::-1]
    qp, kp, vp = padf(q), padf(k), padf(v)
    o_f = gla_chunk_scan(qp, kp, vp, padf(g_f))
    o_b = flip(gla_chunk_scan(flip(qp), flip(kp), flip(vp), flip(padf(g_b))))
    return (o_f + o_b)[:, pad:]


def peer_ffn(x, w_q, sub_keys, u, v):
    N, D = x.shape
    nb = -(-N // PEER_BLOCK)
    xb = jnp.pad(x, ((0, nb * PEER_BLOCK - N), (0, 0))).reshape(nb, PEER_BLOCK, D)

    def block(xi):
        n = xi.shape[0]
        q = (xi @ w_q).reshape(n, PEER_HEADS, 2, PEER_KEY_DIM)
        s = jnp.einsum('nhpc,hpkc->nhpk', q, sub_keys).astype(jnp.float32)
        ts, ti = lax.top_k(s, PEER_TOPK)
        cand_s = (ts[:, :, 0, :, None] + ts[:, :, 1, None, :]).reshape(n, PEER_HEADS, PEER_TOPK * PEER_TOPK)
        cand_i = (ti[:, :, 0, :, None] * PEER_NKEYS + ti[:, :, 1, None, :]).reshape(n, PEER_HEADS, PEER_TOPK * PEER_TOPK)
        best_s, best_p = lax.top_k(cand_s, PEER_TOPK)
        idx = jnp.take_along_axis(cand_i, best_p, axis=-1)
        gate = jax.nn.softmax(best_s, axis=-1)
        act = jax.nn.gelu(jnp.einsum('nd,nhkd->nhk', xi, u[idx]).astype(jnp.float32), approximate=False)
        return jnp.einsum('nhk,nhkd->nd', (gate * act).astype(xi.dtype), v[idx])

    return lax.map(block, xb).reshape(nb * PEER_BLOCK, D)[:N]


def setup_inputs(seed: int = 0) -> dict:
    key = jax.random.key(seed)
    ks = jax.random.split(key, 24)
    nrm = lambda k, shape, s: jax.random.normal(k, shape, jnp.float32) * s
    gain = lambda k, shape: 1.0 + 0.02 * jax.random.normal(k, shape, jnp.float32)
    return {
        'x': nrm(ks[0], (BATCH, SEQ, D_MODEL), 1.0),
        'meta_tokens': nrm(ks[1], (N_META, D_MODEL), 1.0),
        'ln1_w': gain(ks[2], (DEPTH, D_MODEL)),
        'w_in': nrm(ks[3], (DEPTH, D_MODEL, IN_WIDTH), D_MODEL ** -0.5),
        'da_lambda_q1': nrm(ks[4], (DEPTH, DA_HEAD_DIM), 0.1),
        'da_lambda_k1': nrm(ks[5], (DEPTH, DA_HEAD_DIM), 0.1),
        'da_lambda_q2': nrm(ks[6], (DEPTH, DA_HEAD_DIM), 0.1),
        'da_lambda_k2': nrm(ks[7], (DEPTH, DA_HEAD_DIM), 0.1),
        'da_subln_w': gain(ks[8], (DEPTH, DA_V_DIM)),
        'gla_gate_w2_f': nrm(ks[9], (DEPTH, GLA_GATE_RANK, GLA_HEADS * GLA_DK), GLA_GATE_RANK ** -0.5),
        'gla_gate_b_f': nrm(ks[10], (DEPTH, GLA_HEADS * GLA_DK), 0.1),
        'gla_gate_w2_b': nrm(ks[11], (DEPTH, GLA_GATE_RANK, GLA_HEADS * GLA_DK), GLA_GATE_RANK ** -0.5),
        'gla_gate_b_b': nrm(ks[12], (DEPTH, GLA_HEADS * GLA_DK), 0.1),
        'gla_norm_w': gain(ks[13], (DEPTH, GLA_DV)),
        'w_out': nrm(ks[14], (DEPTH, MIX_WIDTH, D_MODEL), MIX_WIDTH ** -0.5),
        'ln2_w': gain(ks[15], (DEPTH, D_MODEL)),
        'peer_w_q': nrm(ks[16], (DEPTH, D_MODEL, PEER_HEADS * 2 * PEER_KEY_DIM), D_MODEL ** -0.5),
        'peer_sub_keys': nrm(ks[17], (DEPTH, PEER_HEADS, 2, PEER_NKEYS, PEER_KEY_DIM), PEER_KEY_DIM ** -0.5),
        'peer_u': nrm(ks[18], (DEPTH, PEER_EXPERTS, D_MODEL), D_MODEL ** -0.5),
        'peer_v': nrm(ks[19], (DEPTH, PEER_EXPERTS, D_MODEL), PEER_HEADS ** -0.5),
        'final_norm_w': gain(ks[20], (D_MODEL,)),
    }


def reference(x, meta_tokens, ln1_w, w_in, da_lambda_q1, da_lambda_k1, da_lambda_q2, da_lambda_k2, da_subln_w, gla_gate_w2_f, gla_gate_b_f, gla_gate_w2_b, gla_gate_b_b, gla_norm_w, w_out, ln2_w, peer_w_q, peer_sub_keys, peer_u, peer_v, final_norm_w):
    B = x.shape[0]
    meta = jnp.broadcast_to(meta_tokens.astype(x.dtype)[None], (B, N_META, D_MODEL))
    h = jnp.concatenate([meta, x], axis=1)
    T = h.shape[1]
    pos = jnp.arange(T)
    splits = np.cumsum(IN_SIZES)[:-1].tolist()
    for l in range(DEPTH):
        xn = rmsnorm(h, ln1_w[l])
        proj = xn @ w_in[l]
        dq, dk, dv, gq, gk, gv, gr, zf, zb = jnp.split(proj, splits, axis=-1)

        lam_init = 0.8 - 0.6 * math.exp(-0.3 * l)
        lam = (jnp.exp(jnp.sum(da_lambda_q1[l].astype(jnp.float32) * da_lambda_k1[l].astype(jnp.float32)))
               - jnp.exp(jnp.sum(da_lambda_q2[l].astype(jnp.float32) * da_lambda_k2[l].astype(jnp.float32))) + lam_init)
        da = diff_attention(dq.reshape(B, T, DA_HEADS, 2, DA_HEAD_DIM), dk.reshape(B, T, DA_HEADS, 2, DA_HEAD_DIM),
                            dv.reshape(B, T, DA_HEADS, DA_V_DIM), lam, pos)
        da = (rmsnorm(da, da_subln_w[l]) * (1.0 - lam_init)).reshape(B, T, DA_WIDTH)

        f32 = jnp.float32
        log_g_f = jax.nn.log_sigmoid((zf @ gla_gate_w2_f[l] + gla_gate_b_f[l]).astype(f32)) / GLA_TAU
        log_g_b = jax.nn.log_sigmoid((zb @ gla_gate_w2_b[l] + gla_gate_b_b[l]).astype(f32)) / GLA_TAU
        go = gla_bidirectional((gq.astype(f32) * (GLA_DK ** -0.5)).reshape(B, T, GLA_HEADS, GLA_DK),
                               gk.astype(f32).reshape(B, T, GLA_HEADS, GLA_DK),
                               gv.astype(f32).reshape(B, T, GLA_HEADS, GLA_DV),
                               log_g_f.reshape(B, T, GLA_HEADS, GLA_DK),
                               log_g_b.reshape(B, T, GLA_HEADS, GLA_DK))
        go = rmsnorm(go, gla_norm_w[l]).reshape(B, T, GLA_WIDTH).astype(h.dtype) * jax.nn.silu(gr)

        h = h + jnp.concatenate([da, go], axis=-1) @ w_out[l]
        if l == DEPTH - 1:
            h = h[:, N_META:]

        hn = rmsnorm(h, ln2_w[l])
        h = h + peer_ffn(hn.reshape(-1, D_MODEL), peer_w_q[l], peer_sub_keys[l], peer_u[l], peer_v[l]).reshape(h.shape)
    return rmsnorm(h, final_norm_w)
```

```python
import functools
import math

import jax
import jax.numpy as jnp
from jax import lax
from jax.experimental import pallas as pl
from jax.experimental.pallas import tpu as pltpu

F32 = jnp.float32
BF16 = jnp.bfloat16

D_MODEL = 1024
N_META = 16
EPS = 1e-6
DA_HEADS = 4
DA_HEAD_DIM = 64
DA_V_DIM = 128
DA_WIDTH = DA_HEADS * DA_V_DIM
ROPE_THETA = 500000.0
ROPE_DIM = DA_HEAD_DIM // 4
ROPE_HALF = ROPE_DIM // 2
GLA_HEADS = 4
GLA_DK = 64
GLA_DV = 128
GLA_QK = GLA_HEADS * GLA_DK
GLA_WIDTH = GLA_HEADS * GLA_DV
GLA_GATE_RANK = 16
GLA_TAU = 16.0
PEER_HEADS = 8
PEER_NKEYS = 128
PEER_EXPERTS = PEER_NKEYS * PEER_NKEYS
PEER_KEY_DIM = 128
PEER_TOPK = 16
LAM_INIT = 0.8 - 0.6 * math.exp(-0.3 * 0)

LANE = 128
IN_USED = 3 * 512 + 2 * 256 + 2 * 512 + 2 * GLA_GATE_RANK
IN_PAD = 3200
COL_DQ, COL_DK, COL_DV, COL_GQ, COL_GK, COL_GV, COL_GR, COL_Z = 0, 512, 1024, 1536, 1792, 2048, 2560, 3072

GLA_CHUNK = 64
GLA_SUB = 16
GLA_NSUB = GLA_CHUNK // GLA_SUB
MASK_NEG = -1e30
VMEM_LIMIT = 56 * 1024 * 1024


def _cparams(sem):
    return pltpu.CompilerParams(dimension_semantics=sem, vmem_limit_bytes=VMEM_LIMIT)


def _dot_nt(a, b):
    return lax.dot_general(a, b, (((1,), (1,)), ((), ())), preferred_element_type=F32)


def _dot_tn(a, b):
    return lax.dot_general(a, b, (((0,), (0,)), ((), ())), preferred_element_type=F32)


def _dot(a, b):
    return jnp.dot(a, b, preferred_element_type=F32)


def _rms(x, w):
    return x * lax.rsqrt(jnp.mean(x * x, axis=-1, keepdims=True) + EPS) * w


def _rope(x, c, s_lo, s_hi):
    n = x.shape[-1]
    c = jnp.concatenate([c] * DA_HEADS, axis=-1)
    s_lo = jnp.concatenate([s_lo] * DA_HEADS, axis=-1)
    s_hi = jnp.concatenate([s_hi] * DA_HEADS, axis=-1)
    up = pltpu.roll(x, n - ROPE_HALF, 1)
    dn = pltpu.roll(x, ROPE_HALF, 1)
    return x * c + up * s_lo + dn * s_hi


def _proj_kernel(x_ref, ln_ref, w_ref, cq_ref, sq1_ref, sq2_ref, ck_ref, sk1_ref, sk2_ref,
                 w2_ref, gbias_ref,
                 dq_ref, dk_ref, dv_ref, gq_ref, gk_ref, gv_ref, gr_ref, gf_ref, gb_ref):
    xn = _rms(x_ref[...], ln_ref[...])
    proj = _dot(xn.astype(BF16), w_ref[...])
    dq = proj[:, COL_DQ:COL_DQ + 512]
    dk = proj[:, COL_DK:COL_DK + 512]
    dq_ref[...] = _rope(dq, cq_ref[...], sq1_ref[...], sq2_ref[...]).astype(BF16)
    dk_ref[...] = _rope(dk, ck_ref[...], sk1_ref[...], sk2_ref[...]).astype(BF16)
    dv_ref[...] = proj[:, COL_DV:COL_DV + 512].astype(BF16)
    gq_ref[...] = proj[:, COL_GQ:COL_GQ + 256] * (GLA_DK ** -0.5)
    gk_ref[...] = proj[:, COL_GK:COL_GK + 256]
    gv_ref[...] = proj[:, COL_GV:COL_GV + 512].astype(BF16)
    gr_ref[...] = proj[:, COL_GR:COL_GR + 512]
    z = proj[:, COL_Z:COL_Z + LANE].astype(BF16)
    pre = _dot(z, w2_ref[...]) + gbias_ref[...]
    logsig = jnp.minimum(pre, 0.0) - jnp.log(1.0 + jnp.exp(-jnp.abs(pre)))
    logsig = logsig * (1.0 / GLA_TAU)
    gf_ref[...] = logsig[:, :GLA_QK]
    gb_ref[...] = logsig[:, GLA_QK:]


def _project(x2, ln_w, w_in_p, tabs, w2, gbias, tm):
    n = x2.shape[0]
    row = lambda w: pl.BlockSpec((tm, w), lambda i: (i, 0))
    nt = tabs[0].shape[0] // tm
    tab = pl.BlockSpec((tm, LANE), lambda i: (i % nt, 0))
    full = lambda a: pl.BlockSpec(a.shape, lambda i: (0,) * a.ndim)
    out_shapes = [
        jax.ShapeDtypeStruct((n, 512), BF16), jax.ShapeDtypeStruct((n, 512), BF16),
        jax.ShapeDtypeStruct((n, 512), BF16), jax.ShapeDtypeStruct((n, 256), F32),
        jax.ShapeDtypeStruct((n, 256), F32), jax.ShapeDtypeStruct((n, 512), BF16),
        jax.ShapeDtypeStruct((n, 512), F32), jax.ShapeDtypeStruct((n, 256), F32),
        jax.ShapeDtypeStruct((n, 256), F32)]
    out_specs = [row(512), row(512), row(512), row(256), row(256), row(512), row(512), row(256), row(256)]
    return pl.pallas_call(
        _proj_kernel,
        grid=(n // tm,),
        in_specs=[row(D_MODEL), full(ln_w), full(w_in_p)] + [tab] * 6 + [full(w2), full(gbias)],
        out_specs=out_specs,
        out_shape=out_shapes,
        compiler_params=_cparams(("arbitrary",)),
        name="in_proj",
    )(x2, ln_w, w_in_p, *tabs, w2, gbias)


def _da_kernel(lq1_ref, lk1_ref, lq2_ref, lk2_ref, subw_ref, q_ref, k_ref, v_ref, km_ref, vm_ref, o_ref):
    lam = (jnp.exp(jnp.sum(lq1_ref[...] * lk1_ref[...], axis=-1, keepdims=True))
           - jnp.exp(jnp.sum(lq2_ref[...] * lk2_ref[...], axis=-1, keepdims=True)) + LAM_INIT)
    q = q_ref[0]
    k = k_ref[0]
    km = km_ref[...]
    lane = lax.broadcasted_iota(jnp.int32, (1, LANE), 1)
    meta_ok = lane < N_META
    a = None
    am = None
    for c in range(2):
        comp = (lane < DA_HEAD_DIM) if c == 0 else (lane >= DA_HEAD_DIM)
        qc = jnp.where(comp, q, jnp.zeros_like(q))
        s = _dot_nt(qc, k)
        sm = jnp.where(meta_ok, _dot_nt(qc, km), MASK_NEG)
        m = jnp.maximum(jnp.max(s, axis=-1, keepdims=True), jnp.max(sm, axis=-1, keepdims=True))
        p = jnp.exp(s - m)
        pm = jnp.exp(sm - m)
        inv = 1.0 / (jnp.sum(p, axis=-1, keepdims=True) + jnp.sum(pm, axis=-1, keepdims=True))
        coef = inv if c == 0 else -lam * inv
        a = p * coef if a is None else a + p * coef
        am = pm * coef if am is None else am + pm * coef
    o = _dot(a.astype(BF16), v_ref[0]) + _dot(am.astype(BF16), vm_ref[...])
    o = _rms(o, subw_ref[...]) * (1.0 - LAM_INIT)
    o_ref[0] = o.astype(BF16)


def _diff_attention(lams, subw, dq, dk, dv, km, vm, tq):
    b, s, _ = dq.shape
    vec = pl.BlockSpec((1, DA_HEAD_DIM), lambda bi, h, i: (0, 0))
    return pl.pallas_call(
        _da_kernel,
        grid=(b, DA_HEADS, s // tq),
        in_specs=[vec, vec, vec, vec,
                  pl.BlockSpec((1, DA_V_DIM), lambda bi, h, i: (0, 0)),
                  pl.BlockSpec((1, tq, LANE), lambda bi, h, i: (bi, i, h)),
                  pl.BlockSpec((1, s, LANE), lambda bi, h, i: (bi, 0, h)),
                  pl.BlockSpec((1, s, LANE), lambda bi, h, i: (bi, 0, h)),
                  pl.BlockSpec((LANE, LANE), lambda bi, h, i: (0, h)),
                  pl.BlockSpec((LANE, LANE), lambda bi, h, i: (0, h))],
        out_specs=pl.BlockSpec((1, tq, LANE), lambda bi, h, i: (bi, i, h)),
        out_shape=jax.ShapeDtypeStruct((b, s, DA_WIDTH), BF16),
        compiler_params=_cparams(("arbitrary", "arbitrary", "arbitrary")),
        name="diff_attn",
    )(*lams, subw, dq, dk, dv, km, vm)


def _gla_cumsum(g, rev):
    c = g.shape[0]
    row = lax.broadcasted_iota(jnp.int32, (c, c), 0)
    col = lax.broadcasted_iota(jnp.int32, (c, c), 1)
    tri = jnp.where((col >= row) if rev else (col <= row), 1.0, 0.0).astype(BF16)
    hi = g.astype(BF16)
    r1 = g - hi.astype(F32)
    mid = r1.astype(BF16)
    lo = (r1 - mid.astype(F32)).astype(BF16)
    return _dot(tri, hi) + _dot(tri, mid) + _dot(tri, lo)


def _gla_state_update(s_ref, k, b, tot, v, bd):
    kd = (k * jnp.exp(tot - b)).astype(BF16)
    s_ref[...] = s_ref[...] * jnp.exp(tot) + _dot_tn(v, kd) * bd


def _gla_chunk(q, k, g, v, s_ref, bd, expand, rev):
    c = GLA_CHUNK
    b = _gla_cumsum(g, rev)
    tot = b[0:1] if rev else b[c - 1:c]
    qe = (q * jnp.exp(b)).astype(BF16)
    o = _dot_nt(qe, s_ref[...].astype(BF16))
    _gla_state_update(s_ref, k, b, tot, v, bd)

    head_of_lane = lax.broadcasted_iota(jnp.int32, (1, GLA_QK), 1) // GLA_DK
    colj = lax.broadcasted_iota(jnp.int32, (1, c), 1)
    sub_i = lax.broadcasted_iota(jnp.int32, (GLA_SUB, 1), 0)
    vf = v.astype(F32)
    a_rows = []
    o_diag = []
    for blk in range(GLA_NSUB):
        lo, hi = blk * GLA_SUB, (blk + 1) * GLA_SUB
        q_i, b_i = q[lo:hi], b[lo:hi]
        has_off = (blk < GLA_NSUB - 1) if rev else (blk > 0)
        if has_off:
            ref_b = b[hi:hi + 1] if rev else b[lo - 1:lo]
            qt = q_i * jnp.exp(b_i - ref_b)
            kt = (k * jnp.exp(jnp.minimum(ref_b - b, 0.0))).astype(BF16)
            qhat = jnp.concatenate(
                [jnp.where(head_of_lane == h, qt, 0.0) for h in range(GLA_HEADS)], axis=0).astype(BF16)
            att = _dot_nt(qhat, kt)
            valid = (colj >= hi) if rev else (colj < lo)
            a_rows.append(jnp.where(valid, att, 0.0))
        else:
            a_rows.append(jnp.zeros((GLA_HEADS * GLA_SUB, c), F32))
        prods = []
        for jl in range(GLA_SUB):
            j = lo + jl
            e = jnp.exp(jnp.minimum(b_i - b[j:j + 1], 0.0))
            ok = (sub_i <= jl) if rev else (sub_i >= jl)
            prods.append(jnp.where(ok, q_i * (k[j:j + 1] * e), 0.0))
        pmat = jnp.concatenate(prods, axis=0).astype(BF16)
        r = _dot(pmat, expand)
        od = None
        for jl in range(GLA_SUB):
            term = r[jl * GLA_SUB:(jl + 1) * GLA_SUB] * vf[lo + jl:lo + jl + 1]
            od = term if od is None else od + term
        o_diag.append(od)
    o_heads = []
    for h in range(GLA_HEADS):
        a_h = jnp.concatenate([a_rows[blk][h * GLA_SUB:(h + 1) * GLA_SUB] for blk in range(GLA_NSUB)], axis=0)
        o_heads.append(_dot(a_h.astype(BF16), v[:, h * GLA_DV:(h + 1) * GLA_DV]))
    return o + jnp.concatenate(o_heads, axis=-1) + jnp.concatenate(o_diag, axis=0)


def _gla_kernel(qf_ref, kf_ref, gf_ref, vf_ref, qb_ref, kb_ref, gb_ref, vb_ref,
                km_ref, gm_ref, vm_ref, bd_ref, ex_ref, of_ref, ob_ref, sf_ref, sb_ref):
    bd = bd_ref[...]
    expand = ex_ref[...]

    @pl.when(pl.program_id(1) == 0)
    def _():
        sf_ref[...] = jnp.zeros_like(sf_ref)
        sb_ref[...] = jnp.zeros_like(sb_ref)
        bm = _gla_cumsum(gm_ref[...], False)
        _gla_state_update(sf_ref, km_ref[...], bm, bm[GLA_CHUNK - 1:GLA_CHUNK], vm_ref[...], bd)

    of_ref[0] = _gla_chunk(qf_ref[0], kf_ref[0], gf_ref[0], vf_ref[0], sf_ref, bd, expand, False)
    ob_ref[0] = _gla_chunk(qb_ref[0], kb_ref[0], gb_ref[0], vb_ref[0], sb_ref, bd, expand, True)


def _gla(gq, gk, gf, gb, gv, km, gm, vm, bd, expand):
    b, s, _ = gq.shape
    nc = s // GLA_CHUNK
    fwd = lambda w: pl.BlockSpec((1, GLA_CHUNK, w), lambda bi, n: (bi, n, 0))
    bwd = lambda w: pl.BlockSpec((1, GLA_CHUNK, w), lambda bi, n: (bi, nc - 1 - n, 0))
    full = lambda a: pl.BlockSpec(a.shape, lambda bi, n: (0,) * a.ndim)
    return pl.pallas_call(
        _gla_kernel,
        grid=(b, nc),
        in_specs=[fwd(GLA_QK), fwd(GLA_QK), fwd(GLA_QK), fwd(GLA_WIDTH),
                  bwd(GLA_QK), bwd(GLA_QK), bwd(GLA_QK), bwd(GLA_WIDTH),
                  full(km), full(gm), full(vm), full(bd), full(expand)],
        out_specs=[fwd(GLA_WIDTH), bwd(GLA_WIDTH)],
        out_shape=[jax.ShapeDtypeStruct((b, s, GLA_WIDTH), F32)] * 2,
        scratch_shapes=[pltpu.VMEM((GLA_WIDTH, GLA_QK), F32), pltpu.VMEM((GLA_WIDTH, GLA_QK), F32)],
        compiler_params=_cparams(("arbitrary", "arbitrary")),
        name="gla",
    )(gq, gk, gf, gv, gq, gk, gb, gv, km, gm, vm, bd, expand)


def _out_kernel(da_ref, of_ref, ob_ref, gr_ref, x_ref, gw_ref, w_ref, h_ref):
    go = of_ref[...] + ob_ref[...]
    gw = gw_ref[...]
    go = jnp.concatenate(
        [_rms(go[:, h * GLA_DV:(h + 1) * GLA_DV], gw) for h in range(GLA_HEADS)], axis=-1)
    gr = gr_ref[...]
    go = go * (gr / (1.0 + jnp.exp(-gr)))
    w = w_ref[...]
    h_ref[...] = x_ref[...] + _dot(da_ref[...], w[:DA_WIDTH]) + _dot(go.astype(BF16), w[DA_WIDTH:])


def _out_proj(da, o_f, o_b, gr, x2, gw, w_out, tm):
    n = x2.shape[0]
    row = lambda w: pl.BlockSpec((tm, w), lambda i: (i, 0))
    full = lambda a: pl.BlockSpec(a.shape, lambda i: (0,) * a.ndim)
    return pl.pallas_call(
        _out_kernel,
        grid=(n // tm,),
        in_specs=[row(512), row(512), row(512), row(512), row(D_MODEL), full(gw), full(w_out)],
        out_specs=row(D_MODEL),
        out_shape=jax.ShapeDtypeStruct((n, D_MODEL), F32),
        compiler_params=_cparams(("arbitrary",)),
        name="out_proj",
    )(da, o_f, o_b, gr, x2, gw, w_out)


def _top16(s):
    tops = []
    cur = s
    for _ in range(PEER_TOPK):
        m = jnp.max(cur, axis=0, keepdims=True)
        tops.append(m)
        cur = jnp.where(cur == m, -jnp.inf, cur)
    return jnp.concatenate(tops, axis=0)


_CAND_LIMITS = [PEER_TOPK // (a + 1) for a in range(8)]


def _peer_prep_kernel(h_ref, ln_ref, wq_ref, keys_ref, hn_ref, st_ref, stats_ref, q_scr):
    head = pl.program_id(1)

    @pl.when(head == 0)
    def _():
        hn = _rms(h_ref[...], ln_ref[...]).astype(BF16)
        hn_ref[...] = hn
        q = _dot(hn, wq_ref[...])
        for hp in range(2 * PEER_HEADS):
            q_scr[hp] = q[:, hp * LANE:(hp + 1) * LANE].astype(BF16)

    s1 = _dot_nt(keys_ref[0, 0], q_scr[2 * head])
    s2 = _dot_nt(keys_ref[0, 1], q_scr[2 * head + 1])
    st_ref[0, 0] = s1
    st_ref[0, 1] = s2
    t1 = _top16(s1)
    t2 = _top16(s2)
    sub8 = lax.broadcasted_iota(jnp.int32, (8, 1), 0)
    groups = [t2 + t1[0:1]]
    for a in range(1, 8):
        grp = t2[0:8] + t1[a:a + 1]
        if _CAND_LIMITS[a] < 8:
            grp = jnp.where(sub8 < _CAND_LIMITS[a], grp, -jnp.inf)
        groups.append(grp)
    groups.append(t1[8:16] + t2[0:1])
    cand = jnp.concatenate(groups, axis=0)
    cur = cand
    for _ in range(PEER_TOPK - 1):
        m = jnp.max(cur, axis=0, keepdims=True)
        cur = jnp.where(cur == m, -jnp.inf, cur)
    tau = jnp.max(cur, axis=0, keepdims=True)
    top = t1[0:1] + t2[0:1]
    z = jnp.sum(jnp.where(cand >= tau, jnp.exp(cand - top), 0.0), axis=0, keepdims=True)
    zero = jnp.zeros_like(tau)
    stats_ref[0] = jnp.concatenate([tau, t1[0:1], t2[0:1], 1.0 / z, zero, zero, zero, zero], axis=0)


def _peer_prep(h2, ln_w, wq, keys, tn):
    n = h2.shape[0]
    return pl.pallas_call(
        _peer_prep_kernel,
        grid=(n // tn, PEER_HEADS),
        in_specs=[pl.BlockSpec((tn, D_MODEL), lambda i, h: (i, 0)),
                  pl.BlockSpec((1, D_MODEL), lambda i, h: (0, 0)),
                  pl.BlockSpec(wq.shape, lambda i, h: (0, 0)),
                  pl.BlockSpec((1, 2, PEER_NKEYS, PEER_KEY_DIM), lambda i, h: (h, 0, 0, 0))],
        out_specs=[pl.BlockSpec((tn, D_MODEL), lambda i, h: (i, 0)),
                   pl.BlockSpec((1, 2, PEER_NKEYS, tn), lambda i, h: (h, 0, 0, i)),
                   pl.BlockSpec((1, 8, tn), lambda i, h: (h, 0, i))],
        out_shape=[jax.ShapeDtypeStruct((n, D_MODEL), BF16),
                   jax.ShapeDtypeStruct((PEER_HEADS, 2, PEER_NKEYS, n), F32),
                   jax.ShapeDtypeStruct((PEER_HEADS, 8, n), F32)],
        scratch_shapes=[pltpu.VMEM((2 * PEER_HEADS, tn, LANE), BF16)],
        compiler_params=_cparams(("arbitrary", "arbitrary")),
        name="peer_prep",
    )(h2, ln_w, wq, keys)


def _peer_kernel(hn_ref, u_ref, vt_ref, st_ref, stats_ref, h_ref, fw_ref, o_ref, acc_ref, e1_ref, e2_ref, *, te):
    j = pl.program_id(1)

    @pl.when(j == 0)
    def _():
        acc_ref[...] = jnp.zeros_like(acc_ref)
        for h in range(PEER_HEADS):
            st = stats_ref[h]
            e1_ref[h] = jnp.exp(st_ref[h, 0] - st[1:2]) * st[3:4]
            e2_ref[h] = jnp.exp(st_ref[h, 1] - st[2:3])

    a = _dot_nt(u_ref[...], hn_ref[...])
    act = 0.5 * a * (1.0 + lax.erf(a * (2.0 ** -0.5)))
    nblk = te // PEER_NKEYS
    w_blocks = []
    for ib in range(nblk):
        i1 = j * nblk + ib
        gate = None
        for h in range(PEER_HEADS):
            thr = stats_ref[h, 0:1, :] - st_ref[h, 0, pl.ds(i1, 1), :]
            coef = e1_ref[h, pl.ds(i1, 1), :]
            term = jnp.where(st_ref[h, 1] >= thr, e2_ref[h], 0.0) * coef
            gate = term if gate is None else gate + term
        w_blocks.append((gate * act[ib * PEER_NKEYS:(ib + 1) * PEER_NKEYS]).astype(BF16))
    wmat = jnp.concatenate(w_blocks, axis=0)
    acc_ref[...] += _dot(vt_ref[...], wmat)

    @pl.when(j == pl.num_programs(1) - 1)
    def _():
        o_ref[...] = _rms(h_ref[...] + acc_ref[...].T, fw_ref[...])


def _peer(hn, u, vt, st, stats, h2, fw, tn, te):
    n = hn.shape[0]
    return pl.pallas_call(
        functools.partial(_peer_kernel, te=te),
        grid=(n // tn, PEER_EXPERTS // te),
        in_specs=[pl.BlockSpec((tn, D_MODEL), lambda i, j: (i, 0)),
                  pl.BlockSpec((te, D_MODEL), lambda i, j: (j, 0)),
                  pl.BlockSpec((D_MODEL, te), lambda i, j: (0, j)),
                  pl.BlockSpec((PEER_HEADS, 2, PEER_NKEYS, tn), lambda i, j: (0, 0, 0, i)),
                  pl.BlockSpec((PEER_HEADS, 8, tn), lambda i, j: (0, 0, i)),
                  pl.BlockSpec((tn, D_MODEL), lambda i, j: (i, 0)),
                  pl.BlockSpec((1, D_MODEL), lambda i, j: (0, 0))],
        out_specs=pl.BlockSpec((tn, D_MODEL), lambda i, j: (i, 0)),
        out_shape=jax.ShapeDtypeStruct((n, D_MODEL), F32),
        scratch_shapes=[pltpu.VMEM((D_MODEL, tn), F32),
                        pltpu.VMEM((PEER_HEADS, PEER_NKEYS, tn), F32),
                        pltpu.VMEM((PEER_HEADS, PEER_NKEYS, tn), F32)],
        compiler_params=_cparams(("arbitrary", "arbitrary")),
        name="peer_dense",
    )(hn, u, vt, st, stats, h2, fw)


def _rope_tables(pos, scale):
    inv = jnp.power(ROPE_THETA, -jnp.arange(0, ROPE_DIM, 2, dtype=F32) / ROPE_DIM)
    ang = pos.astype(F32)[:, None] * inv[None, :]
    cos, sin = jnp.cos(ang), jnp.sin(ang)
    t = pos.shape[0]
    ones = jnp.ones((t, DA_HEAD_DIM - ROPE_DIM), F32)
    zeros8 = jnp.zeros((t, ROPE_HALF), F32)
    zeros_rest = jnp.zeros((t, DA_HEAD_DIM - ROPE_DIM), F32)
    c = jnp.concatenate([cos, cos, ones], axis=-1)
    s_lo = jnp.concatenate([-sin, zeros8, zeros_rest], axis=-1)
    s_hi = jnp.concatenate([zeros8, sin, zeros_rest], axis=-1)
    two = lambda a: jnp.concatenate([a, a], axis=-1) * scale
    return two(c), two(s_lo), two(s_hi)


def kernel(x, meta_tokens, ln1_w, w_in, da_lambda_q1, da_lambda_k1, da_lambda_q2, da_lambda_k2, da_subln_w,
           gla_gate_w2_f, gla_gate_b_f, gla_gate_w2_b, gla_gate_b_b, gla_norm_w, w_out, ln2_w,
           peer_w_q, peer_sub_keys, peer_u, peer_v, final_norm_w):
    bsz, seq, d = x.shape
    n = bsz * seq
    x2 = x.reshape(n, d)

    w_in_p = jnp.pad(w_in[0], ((0, 0), (0, IN_PAD - IN_USED))).astype(BF16)
    w2 = jnp.zeros((LANE, 2 * GLA_QK), F32)
    w2 = w2.at[0:GLA_GATE_RANK, 0:GLA_QK].set(gla_gate_w2_f[0])
    w2 = w2.at[GLA_GATE_RANK:2 * GLA_GATE_RANK, GLA_QK:].set(gla_gate_w2_b[0]).astype(BF16)
    gbias = jnp.concatenate([gla_gate_b_f[0], gla_gate_b_b[0]])[None, :]
    ln1 = ln1_w[0][None, :]
    q_scale = DA_HEAD_DIM ** -0.5
    pos_x = jnp.arange(N_META, N_META + seq)
    pos_m = jnp.arange(N_META)
    tabs_x = _rope_tables(pos_x, q_scale) + _rope_tables(pos_x, 1.0)
    tabs_m = _rope_tables(pos_m, q_scale) + _rope_tables(pos_m, 1.0)

    dq, dk, dv, gq, gk, gv, gr, gf, gb = _project(x2, ln1, w_in_p, tabs_x, w2, gbias, 512)
    _, mdk, mdv, _, mgk, mgv, _, mgf, _ = _project(meta_tokens, ln1, w_in_p, tabs_m, w2, gbias, N_META)

    pad_rows = lambda a, top, bot: jnp.pad(a, ((top, bot), (0, 0)))
    km = pad_rows(mdk, 0, LANE - N_META)
    vm = pad_rows(mdv, 0, LANE - N_META)
    lams = [p[0][None, :] for p in (da_lambda_q1, da_lambda_k1, da_lambda_q2, da_lambda_k2)]
    r3 = lambda a: a.reshape(bsz, seq, a.shape[-1])
    da = _diff_attention(lams, da_subln_w[0][None, :], r3(dq), r3(dk), r3(dv), km, vm, 512)

    gpad = GLA_CHUNK - N_META
    bd = (jnp.arange(GLA_WIDTH)[:, None] // GLA_DV == jnp.arange(GLA_QK)[None, :] // GLA_DK).astype(F32)
    expand = bd.T.astype(BF16)
    o_f, o_b = _gla(r3(gq), r3(gk), r3(gf), r3(gb), r3(gv),
                    pad_rows(mgk, gpad, 0), pad_rows(mgf, gpad, 0), pad_rows(mgv, gpad, 0), bd, expand)

    h2 = _out_proj(da.reshape(n, DA_WIDTH), o_f.reshape(n, GLA_WIDTH), o_b.reshape(n, GLA_WIDTH), gr, x2,
                   gla_norm_w[0][None, :], w_out[0].astype(BF16), 512)

    keys = peer_sub_keys[0].astype(BF16)
    hn, st, stats = _peer_prep(h2, ln2_w[0][None, :], peer_w_q[0].astype(BF16), keys, 256)
    out = _peer(hn, peer_u[0].astype(BF16), peer_v[0].T.astype(BF16), st, stats, h2,
                final_norm_w[None, :], 512, 512)
    return out.reshape(bsz, seq, d)
```
